```python
import math
import jax, jax.numpy as jnp
from jax import lax
import numpy as np

D_MODEL = 1024
BATCH = 2
SEQ = 8192
DEPTH = 4
DEC_BATCH = 32
DEC_SEQ = 1
PAST_LEN = 8192
PAGE_SIZE = 128

N_BRANCH = 4
BRANCH_W = D_MODEL // N_BRANCH
HEAD_DIM = 64
H_A = BRANCH_W // HEAD_DIM
H_B = 4
DV_B = BRANCH_W // H_B
DK_B = DV_B // 2
GLA_RANK = 16
GLA_TAU = 16.0
GLA_CHUNK = 64
H_C = BRANCH_W // HEAD_DIM
H_IDX = 8
D_IDX = 64
TOPK_MAX = 256
H_D = BRANCH_W // HEAD_DIM
DH_D = HEAD_DIM // 2
N_BUCKETS = 32
T5_MAX_DIST = 128
Q_BLOCK = 128
EPS = 1e-6
FOX_GATE_OFFSET = 2.0

SPLITS = (
    ('fox_q', H_A * HEAD_DIM), ('fox_k', H_A * HEAD_DIM), ('fox_v', H_A * HEAD_DIM), ('fox_f', H_A),
    ('gla_q', H_B * DK_B), ('gla_k', H_B * DK_B), ('gla_v', H_B * DV_B), ('gla_g', GLA_RANK),
    ('dsa_q', H_C * HEAD_DIM), ('dsa_k', H_C * HEAD_DIM), ('dsa_v', H_C * HEAD_DIM),
    ('idx_q', H_IDX * D_IDX), ('idx_k', D_IDX), ('idx_w', H_IDX),
    ('diff_q', H_D * HEAD_DIM), ('diff_k', H_D * HEAD_DIM), ('diff_v', H_D * HEAD_DIM),
    ('z', N_BRANCH * BRANCH_W), ('gate', N_BRANCH * D_MODEL),
)
D_IN = sum(w for _, w in SPLITS)

kernel_name = 'hybrid_fox_gla_dsa_diff_step'


def rmsnorm(x, g):
    xf = x.astype(jnp.float32)
    y = xf * lax.rsqrt(jnp.mean(xf * xf, axis=-1, keepdims=True) + EPS)
    return (y * g.astype(jnp.float32)).astype(x.dtype)


def split_proj(h, w):
    u = jnp.einsum('btd,de->bte', h, w)
    out, off = {}, 0
    for name, width in SPLITS:
        out[name] = u[..., off:off + width]
        off += width
    return out


def t5_bucket(rel):
    n = jnp.maximum(rel, 0)
    n_exact = N_BUCKETS // 2
    n_log = jnp.log(jnp.maximum(n, n_exact).astype(jnp.float32) / n_exact) / math.log(T5_MAX_DIST / n_exact)
    large = jnp.minimum(n_exact + (n_log * (N_BUCKETS - n_exact)).astype(jnp.int32), N_BUCKETS - 1)
    return jnp.where(n < n_exact, n, large)


def over_query_blocks(fn, pos_q, *q_arrays):
    T = pos_q.shape[0]
    if T <= Q_BLOCK or T % Q_BLOCK:
        return fn(pos_q, *q_arrays)
    nb = T // Q_BLOCK
    blk = lambda a: jnp.moveaxis(a.reshape(a.shape[0], nb, Q_BLOCK, *a.shape[2:]), 1, 0)
    out = lax.map(lambda xs: fn(*xs), (pos_q.reshape(nb, Q_BLOCK), *[blk(a) for a in q_arrays]))
    return jnp.moveaxis(out, 0, 1).reshape(out.shape[1], T, *out.shape[3:])


def fox_attend(pos_q, q, cq, k, v, ck, pos_k):
    s = jnp.einsum('bthd,bshd->bhts', q, k).astype(jnp.float32) * HEAD_DIM ** -0.5
    s = s + (jnp.moveaxis(cq, 2, 1)[..., :, None] - jnp.moveaxis(ck, 2, 1)[..., None, :])
    mask = pos_k[None, :] <= pos_q[:, None]
    p = jax.nn.softmax(jnp.where(mask, s, -jnp.inf), axis=-1)
    return jnp.einsum('bhts,bshd->bthd', p.astype(v.dtype), v)


def dsa_attend(pos_q, q, qi, wi, k, v, ki, pos_k, table, n_sel):
    s_idx = jnp.einsum('bthd,bsd->bths', qi, ki) * D_IDX ** -0.5
    score = jnp.einsum('bth,bths->bts', wi, jax.nn.relu(s_idx)).astype(jnp.float32)
    mask = pos_k[None, :] <= pos_q[:, None]
    score = jnp.where(mask[None], score, -jnp.inf)
    _, idx = lax.top_k(score, n_sel)
    gather = jax.vmap(lambda a, i: a[i])
    k_sel, v_sel = gather(k, idx), gather(v, idx)
    rel = pos_q[None, :, None] - pos_k[idx]
    bias = jnp.swapaxes(table[t5_bucket(rel)], -1, -2)
    s = jnp.einsum('bthd,btkhd->bthk', q, k_sel).astype(jnp.float32) * HEAD_DIM ** -0.5 + bias
    p = jax.nn.softmax(jnp.where((rel >= 0)[:, :, None, :], s, -jnp.inf), axis=-1)
    return jnp.einsum('bthk,btkhd->bthd', p.astype(v.dtype), v_sel)


def diff_attend(pos_q, q, k, v, pos_k, table, lam):
    q1, q2 = q[..., :DH_D], q[..., DH_D:]
    k1, k2 = k[..., :DH_D], k[..., DH_D:]
    bias = jnp.moveaxis(table[t5_bucket(pos_q[:, None] - pos_k[None, :])], -1, 0)
    mask = pos_k[None, :] <= pos_q[:, None]

    def amap(qa, ka):
        s = jnp.einsum('bthd,bshd->bhts', qa, ka).astype(jnp.float32) * DH_D ** -0.5 + bias
        return jax.nn.softmax(jnp.where(mask, s, -jnp.inf), axis=-1)

    p = amap(q1, k1) - lam * amap(q2, k2)
    return jnp.einsum('bhts,bshd->bthd', p.astype(v.dtype), v)


def gla_chunk(S, q, k, v, loga):
    C = q.shape[1]
    b = jnp.cumsum(loga, axis=1)
    causal = jnp.tril(jnp.ones((C, C), bool))
    seg = jnp.where(causal[None, :, :, None, None], b[:, :, None] - b[:, None, :], -jnp.inf)
    A = jnp.einsum('bthd,bshd,btshd->bhts', q, k, jnp.exp(seg))
    o = jnp.einsum('bhts,bshv->bthv', A, v) + jnp.einsum('bthd,bhdv->bthv', q * jnp.exp(b), S)
    b_last = b[:, -1]
    S = jnp.exp(b_last)[..., None] * S + jnp.einsum('bshd,bshv->bhdv', k * jnp.exp(b_last[:, None] - b), v)
    return S, o


def gla(q, k, v, loga, S0):
    dt, sdt = v.dtype, S0.dtype
    q, k, v, loga, S = (a.astype(jnp.float32) for a in (q, k, v, loga, S0))
    T = q.shape[1]
    if T <= GLA_CHUNK or T % GLA_CHUNK:
        S, o = gla_chunk(S, q, k, v, loga)
    else:
        nc = T // GLA_CHUNK
        blk = lambda a: jnp.moveaxis(a.reshape(a.shape[0], nc, GLA_CHUNK, *a.shape[2:]), 1, 0)
        S, o = lax.scan(lambda s, xs: gla_chunk(s, *xs), S, (blk(q), blk(k), blk(v), blk(loga)))
        o = jnp.moveaxis(o, 0, 1).reshape(q.shape[0], T, *o.shape[3:])
    return S.astype(sdt), o.astype(dt)


def gather_pages(cache, l, page_table):
    rows = cache[l, page_table]
    return rows.reshape(rows.shape[0], rows.shape[1] * rows.shape[2], *rows.shape[3:])


def layer(x, pos_q, pos_k, past, n_sel, lam_init, rel_bias, lw):
    (g_pre, g_post, w_in, b_f, w_gate, b_gate, g_gla, lq1, lk1, lq2, lk2, g_diff, w_branch, w_out) = lw
    B, T, _ = x.shape
    heads = lambda a, n: a.reshape(B, T, n, -1)
    p = split_proj(rmsnorm(x, g_pre), w_in)
    fq, fk, fv = heads(p['fox_q'], H_A), heads(p['fox_k'], H_A), heads(p['fox_v'], H_A)
    logf = jax.nn.log_sigmoid((p['fox_f'] + b_f).astype(jnp.float32))
    gq = heads(p['gla_q'], H_B) * DK_B ** -0.5
    gk, gv = heads(p['gla_k'], H_B), heads(p['gla_v'], H_B)
    loga = heads(jax.nn.log_sigmoid(jnp.einsum('btr,re->bte', p['gla_g'], w_gate) + b_gate) / GLA_TAU, H_B)
    cq, ck, cv = heads(p['dsa_q'], H_C), heads(p['dsa_k'], H_C), heads(p['dsa_v'], H_C)
    iq, ik, iw = heads(p['idx_q'], H_IDX), p['idx_k'], p['idx_w'] * H_IDX ** -0.5
    dq, dk, dv = heads(p['diff_q'], H_D), heads(p['diff_k'], H_D), heads(p['diff_v'], H_D)

    new = (fk, fv, logf, ck, cv, ik, dk, dv)
    if past is None:
        keys, s0 = new, jnp.zeros((B, H_B, DK_B, DV_B), jnp.float32)
    else:
        keys = tuple(jnp.concatenate([old.astype(nw.dtype), nw], axis=1) for old, nw in zip(past[:-1], new))
        s0 = past[-1]
    kf, vf, logf_all, kc, vc, kidx, kd, vd = keys
    c_all = jnp.cumsum(logf_all, axis=1)

    o_a = over_query_blocks(lambda pq, q, cqb: fox_attend(pq, q, cqb, kf, vf, c_all, pos_k),
                            pos_q, fq, c_all[:, -T:])
    s_gla, o_b = gla(gq, gk, gv, loga, s0)
    o_b = rmsnorm(o_b, g_gla)
    o_c = over_query_blocks(lambda pq, q, qi, wi: dsa_attend(pq, q, qi, wi, kc, vc, kidx, pos_k,
                                                             rel_bias[:, :H_C], n_sel),
                            pos_q, cq, iq, iw)
    lam = (jnp.exp(jnp.sum(lq1.astype(jnp.float32) * lk1.astype(jnp.float32)))
           - jnp.exp(jnp.sum(lq2.astype(jnp.float32) * lk2.astype(jnp.float32))) + lam_init)
    o_d = over_query_blocks(lambda pq, q: diff_attend(pq, q, kd, vd, pos_k, rel_bias[:, H_C:], lam), pos_q, dq)
    o_d = rmsnorm(o_d, g_diff) * (1.0 - lam_init)

    o_all = jnp.stack([o.reshape(B, T, BRANCH_W) for o in (o_a, o_b, o_c, o_d)], axis=2)
    z = p['z'].reshape(B, T, N_BRANCH, BRANCH_W)
    y_br = jnp.einsum('btnc,ncd->btnd', o_all * jax.nn.silu(z), w_branch)
    gates = jax.nn.sigmoid(p['gate'].reshape(B, T, N_BRANCH, D_MODEL))
    out = jnp.einsum('btd,de->bte', jnp.sum(gates * y_br, axis=2), w_out)
    return x + rmsnorm(out, g_post), new + (s_gla,)


def setup_inputs(seed: int = 0) -> dict:
    key = jax.random.key(seed)
    ks = jax.random.split(key, 32)
    n_pages = PAST_LEN // PAGE_SIZE
    n_used = DEC_BATCH * n_pages
    n_pool = n_used + n_used // 4
    nrm = lambda k, shape, scale=1.0: scale * jax.random.normal(k, shape, jnp.float32)
    pool = (DEPTH, n_pool, PAGE_SIZE)
    page_table = jax.random.permutation(ks[0], n_pool)[:n_used].reshape(DEC_BATCH, n_pages).astype(jnp.int32)
    return {
        'x_prompt': nrm(ks[1], (BATCH, SEQ, D_MODEL)),
        'x_sample': nrm(ks[2], (DEC_BATCH, DEC_SEQ, D_MODEL)),
        'cache_fox_k': nrm(ks[3], pool + (H_A, HEAD_DIM)),
        'cache_fox_v': nrm(ks[4], pool + (H_A, HEAD_DIM)),
        'cache_fox_logf': jax.nn.log_sigmoid(FOX_GATE_OFFSET + nrm(ks[5], pool + (H_A,))),
        'cache_dsa_k': nrm(ks[6], pool + (H_C, HEAD_DIM)),
        'cache_dsa_v': nrm(ks[7], pool + (H_C, HEAD_DIM)),
        'cache_dsa_idxk': nrm(ks[8], pool + (D_IDX,)),
        'cache_diff_k': nrm(ks[9], pool + (H_D, HEAD_DIM)),
        'cache_diff_v': nrm(ks[10], pool + (H_D, HEAD_DIM)),
        'state_gla': nrm(ks[11], (DEPTH, DEC_BATCH, H_B, DK_B, DV_B), 0.5),
        'page_table': page_table,
        'norm_pre': 1.0 + nrm(ks[12], (DEPTH, D_MODEL), 0.1),
        'norm_post': 1.0 + nrm(ks[13], (DEPTH, D_MODEL), 0.1),
        'w_in': nrm(ks[14], (DEPTH, D_MODEL, D_IN), D_MODEL ** -0.5),
        'fox_b_f': FOX_GATE_OFFSET + nrm(ks[15], (DEPTH, H_A), 0.5),
        'gla_w_gate': nrm(ks[16], (DEPTH, GLA_RANK, H_B * DK_B), GLA_RANK ** -0.5),
        'gla_b_gate': nrm(ks[17], (DEPTH, H_B * DK_B), 0.1),
        'gla_norm': 1.0 + nrm(ks[18], (DEPTH, DV_B), 0.1),
        'diff_lambda_q1': nrm(ks[19], (DEPTH, DH_D), 0.1),
        'diff_lambda_k1': nrm(ks[20], (DEPTH, DH_D), 0.1),
        'diff_lambda_q2': nrm(ks[21], (DEPTH, DH_D), 0.1),
        'diff_lambda_k2': nrm(ks[22], (DEPTH, DH_D), 0.1),
        'diff_norm': 1.0 + nrm(ks[23], (DEPTH, HEAD_DIM), 0.1),
        'w_branch': nrm(ks[24], (DEPTH, N_BRANCH, BRANCH_W, D_MODEL), BRANCH_W ** -0.5),
        'w_out': nrm(ks[25], (DEPTH, D_MODEL, D_MODEL), D_MODEL ** -0.5),
        'rel_bias': nrm(ks[26], (N_BUCKETS, H_C + H_D), 0.5),
    }


def reference(x_prompt, x_sample, cache_fox_k, cache_fox_v, cache_fox_logf, cache_dsa_k, cache_dsa_v,
              cache_dsa_idxk, cache_diff_k, cache_diff_v, state_gla, page_table, norm_pre, norm_post, w_in,
              fox_b_f, gla_w_gate, gla_b_gate, gla_norm, diff_lambda_q1, diff_lambda_k1, diff_lambda_q2,
              diff_lambda_k2, diff_norm, w_branch, w_out, rel_bias):
    pos_p = jnp.arange(SEQ, dtype=jnp.int32)
    pos_sq = PAST_LEN + jnp.arange(DEC_SEQ, dtype=jnp.int32)
    pos_sk = jnp.arange(PAST_LEN + DEC_SEQ, dtype=jnp.int32)
    n_sel_p = min(TOPK_MAX, SEQ // 4)
    n_sel_s = min(TOPK_MAX, (PAST_LEN + DEC_SEQ) // 4)
    caches = (cache_fox_k, cache_fox_v, cache_fox_logf, cache_dsa_k, cache_dsa_v, cache_dsa_idxk,
              cache_diff_k, cache_diff_v)
    xp, xs = x_prompt, x_sample
    rows_p, rows_s = [], []
    for l in range(DEPTH):
        lw = (norm_pre[l], norm_post[l], w_in[l], fox_b_f[l], gla_w_gate[l], gla_b_gate[l], gla_norm[l],
              diff_lambda_q1[l], diff_lambda_k1[l], diff_lambda_q2[l], diff_lambda_k2[l], diff_norm[l],
              w_branch[l], w_out[l])
        lam_init = 0.8 - 0.6 * math.exp(-0.3 * l)
        past = tuple(gather_pages(c, l, page_table) for c in caches) + (state_gla[l],)
        xp, new_p = layer(xp, pos_p, pos_p, None, n_sel_p, lam_init, rel_bias, lw)
        xs, new_s = layer(xs, pos_sq, pos_sk, past, n_sel_s, lam_init, rel_bias, lw)
        rows_p.append(new_p)
        rows_s.append(new_s)
    sp = [jnp.stack(r) for r in zip(*rows_p)]
    ss = [jnp.stack(r) for r in zip(*rows_s)]
    return (xp, xs, sp[0], sp[1], sp[2], sp[3], sp[4], sp[5], sp[6], sp[7], sp[8],
            ss[0], ss[1], ss[2], ss[3], ss[4], ss[5], ss[6], ss[7], ss[8])
```

```python
import functools
import math

import jax
import jax.numpy as jnp
from jax import lax
from jax.experimental import pallas as pl
from jax.experimental.pallas import tpu as pltpu

F32 = jnp.float32
BF16 = jnp.bfloat16
I32 = jnp.int32

N_BRANCH = 4
HEAD_DIM = 64
N_HEADS = 4
BRANCH_W = N_HEADS * HEAD_DIM
H_B = 4
DK_B = 32
DV_B = 64
GLA_RANK = 16
GLA_TAU = 16.0
H_IDX = 8
D_IDX = 64
TOPK_MAX = 256
DH_D = HEAD_DIM // 2
N_BUCKETS = 32
T5_MAX_DIST = 128
EPS = 1e-6

LANES = 128
GLA_BLOCK = 16
NEG = -1e30
INT_MIN = -2 ** 31
VMEM_LIMIT = 56 * 1024 * 1024

_PACK = {}
_off = 0
for _name, _w in (('fox_q', 256), ('fox_k', 256), ('fox_v', 256), ('gla_q', 128), ('gla_k', 128),
                  ('gla_v', 256), ('dsa_q', 256), ('dsa_k', 256), ('dsa_v', 256), ('idx_q', 512),
                  ('idx_k2', 128), ('diff_q', 256), ('diff_k', 256), ('diff_v', 256), ('small', 128)):
    _PACK[_name] = (_off, _w)
    _off += _w
PACK_W = _off
SMALL_FOX_F = 0
SMALL_GLA_G = 4
SMALL_IDX_W = 20

_SPLITS = (('fox_q', 256), ('fox_k', 256), ('fox_v', 256), ('fox_f', 4), ('gla_q', 128), ('gla_k', 128),
           ('gla_v', 256), ('gla_g', GLA_RANK), ('dsa_q', 256), ('dsa_k', 256), ('dsa_v', 256),
           ('idx_q', H_IDX * D_IDX), ('idx_k', D_IDX), ('idx_w', H_IDX), ('diff_q', 256), ('diff_k', 256),
           ('diff_v', 256), ('z', N_BRANCH * BRANCH_W), ('gate', N_BRANCH * 1024))


def _mm(a, b):
    return jnp.dot(a, b, preferred_element_type=F32)


def _mm_nt(a, b):
    return lax.dot_general(a, b, (((1,), (1,)), ((), ())), preferred_element_type=F32)


def _mm_exact(a, b):
    return jnp.dot(a, b, preferred_element_type=F32, precision=lax.Precision.HIGHEST)


def _log_sigmoid(x):
    return jnp.minimum(x, 0.0) - jnp.log(1.0 + jnp.exp(-jnp.abs(x)))


def _sigmoid(x):
    return 1.0 / (1.0 + jnp.exp(-x))


def _rms(x, g):
    return x * lax.rsqrt(jnp.mean(x * x, axis=-1, keepdims=True) + EPS) * g


def _sort_key(score):
    bits = lax.bitcast_convert_type(score + 0.0, I32)
    return jnp.where(bits >= 0, bits, bits ^ jnp.int32(0x7FFFFFFF))


def _params(sem):
    return pltpu.CompilerParams(dimension_semantics=sem, vmem_limit_bytes=VMEM_LIMIT)


def _proj_kernel(x_ref, g_ref, w_ref, o_ref, *, col_chunk):
    xn = _rms(x_ref[...], g_ref[...]).astype(BF16)
    for j in range(0, w_ref.shape[1], col_chunk):
        o_ref[:, j:j + col_chunk] = _mm(xn, w_ref[:, j:j + col_chunk])


def _proj(x2d, g, w, tm):
    n, d = x2d.shape
    e = w.shape[1]
    return pl.pallas_call(
        functools.partial(_proj_kernel, col_chunk=512),
        grid=(n // tm,),
        in_specs=[pl.BlockSpec((tm, d), lambda i: (i, 0)),
                  pl.BlockSpec((1, d), lambda i: (0, 0)),
                  pl.BlockSpec((d, e), lambda i: (0, 0))],
        out_specs=pl.BlockSpec((tm, e), lambda i: (i, 0)),
        out_shape=jax.ShapeDtypeStruct((n, e), F32),
        compiler_params=_params(("arbitrary",)),
        name="proj",
    )(x2d, g.reshape(1, d), w)


def _foxgate_kernel(f_ref, b_ref, lf_ref, c_ref):
    t = f_ref.shape[2]
    lf_ref[0] = _log_sigmoid(f_ref[0] + b_ref[...])
    r = lax.broadcasted_iota(I32, (LANES, LANES), 0)
    c = lax.broadcasted_iota(I32, (LANES, LANES), 1)
    upper = (r <= c).astype(F32)

    def body(j, carry):
        off = pl.multiple_of(j * LANES, LANES)
        cs = _mm_exact(lf_ref[0, :, pl.ds(off, LANES)], upper) + carry
        c_ref[0, :, pl.ds(off, LANES)] = cs
        return cs[:, LANES - 1:LANES]

    lax.fori_loop(0, t // LANES, body, jnp.zeros((8, 1), F32))


def _foxgate(f_t, b8):
    b, _, t = f_t.shape
    spec = pl.BlockSpec((1, 8, t), lambda i: (i, 0, 0))
    return pl.pallas_call(
        _foxgate_kernel,
        grid=(b,),
        in_specs=[spec, pl.BlockSpec((8, 1), lambda i: (0, 0))],
        out_specs=[spec, spec],
        out_shape=[jax.ShapeDtypeStruct((b, 8, t), F32)] * 2,
        compiler_params=_params(("arbitrary",)),
        name="foxgate",
    )(f_t, b8)


def _gla_kernel(q_ref, k_ref, v_ref, sm_ref, wg_ref, bg_ref, gn_ref, s0_ref, o_ref, sout_ref, s_ref,
                *, tm, t_valid):
    i = pl.program_id(1)
    hk = H_B * DK_B
    hv = H_B * DV_B

    @pl.when(i == 0)
    def _():
        s_ref[...] = jnp.zeros((hk, hv), F32)
        for h in range(H_B):
            s_ref[h * DK_B:(h + 1) * DK_B, h * DV_B:(h + 1) * DV_B] = s0_ref[0, h]

    q = q_ref[0] * (DK_B ** -0.5)
    k = k_ref[0]
    v = v_ref[0]
    loga = _log_sigmoid(_mm_exact(sm_ref[0], wg_ref[...]) + bg_ref[...]) * (1.0 / GLA_TAU)
    if t_valid is not None:
        valid = (i * tm + lax.broadcasted_iota(I32, (tm, 1), 0)) < t_valid
        loga = jnp.where(valid, loga, 0.0)
        k = jnp.where(valid, k, 0.0)

    r = lax.broadcasted_iota(I32, (tm, tm), 0)
    c = lax.broadcasted_iota(I32, (tm, tm), 1)
    same = (r // GLA_BLOCK) == (c // GLA_BLOCK)
    bl = _mm_exact((same & (c <= r)).astype(F32), loga)
    blast = _mm_exact(same.astype(F32), loga)
    qe = (q * jnp.exp(bl)).astype(BF16)
    ke_t = (k * jnp.exp(blast - bl)).T
    dec_t = jnp.exp(blast).T

    nb = tm // GLA_BLOCK
    q3 = q.reshape(nb, GLA_BLOCK, hk)
    k3 = k.reshape(nb, GLA_BLOCK, hk)
    b3 = bl.reshape(nb, GLA_BLOCK, hk)
    v3 = v.reshape(nb, GLA_BLOCK, hv)
    trow = lax.broadcasted_iota(I32, (nb, GLA_BLOCK, hk), 1)
    head_of = (lax.broadcasted_iota(I32, (hk, hv), 0) // DK_B) == (lax.broadcasted_iota(I32, (hk, hv), 1) // DV_B)
    expand = head_of.astype(BF16)
    o_diag = jnp.zeros((tm, hv), F32)
    for j in range(GLA_BLOCK):
        pj = q3 * k3[:, j:j + 1, :] * jnp.exp(jnp.minimum(b3 - b3[:, j:j + 1, :], 0.0))
        pj = jnp.where(trow >= j, pj, 0.0)
        aj = _mm(pj.reshape(tm, hk).astype(BF16), expand)
        o_diag = o_diag + (aj.reshape(nb, GLA_BLOCK, hv) * v3[:, j:j + 1, :]).reshape(tm, hv)

    col_blk = lax.broadcasted_iota(I32, (hk, tm), 1) // GLA_BLOCK
    vb = v.astype(BF16)
    s = s_ref[...]
    for n in range(nb):
        rows = slice(n * GLA_BLOCK, (n + 1) * GLA_BLOCK)
        o_ref[0, rows, :] = _mm(qe[rows], s.astype(BF16))
        u = _mm(jnp.where(col_blk == n, ke_t, 0.0).astype(BF16), vb)
        s = dec_t[:, n * GLA_BLOCK:n * GLA_BLOCK + 1] * s + jnp.where(head_of, u, 0.0)
    s_ref[...] = s

    o = o_ref[0] + o_diag
    seg = (lax.broadcasted_iota(I32, (hv, hv), 0) // DV_B) == (lax.broadcasted_iota(I32, (hv, hv), 1) // DV_B)
    ms = _mm_exact(o * o, seg.astype(F32) * (1.0 / DV_B))
    o_ref[0] = o * lax.rsqrt(ms + EPS) * gn_ref[...]

    @pl.when(i == pl.num_programs(1) - 1)
    def _():
        for h in range(H_B):
            sout_ref[0, h] = s_ref[h * DK_B:(h + 1) * DK_B, h * DV_B:(h + 1) * DV_B]


def _gla(q, k, v, small, wg_pad, bg, gnorm, s0, tm, t_valid):
    b, t, _ = q.shape
    hk, hv = H_B * DK_B, H_B * DV_B
    row = lambda w: pl.BlockSpec((1, tm, w), lambda bi, i: (bi, i, 0))
    const = lambda shape: pl.BlockSpec(shape, lambda bi, i: (0,) * len(shape))
    st = pl.BlockSpec((1, H_B, DK_B, DV_B), lambda bi, i: (bi, 0, 0, 0))
    return pl.pallas_call(
        functools.partial(_gla_kernel, tm=tm, t_valid=t_valid),
        grid=(b, t // tm),
        in_specs=[row(hk), row(hk), row(hv), row(LANES), const((LANES, hk)), const((1, hk)), const((1, hv)), st],
        out_specs=[row(hv), st],
        out_shape=[jax.ShapeDtypeStruct((b, t, hv), F32), jax.ShapeDtypeStruct((b, H_B, DK_B, DV_B), F32)],
        scratch_shapes=[pltpu.VMEM((hk, hv), F32)],
        compiler_params=_params(("arbitrary", "arbitrary")),
        name="gla",
    )(q, k, v, small, wg_pad, bg, gnorm, s0)


def _lane_window(lo, width):
    lane = lax.broadcasted_iota(I32, (1, LANES), 1)
    return (lane >= lo) & (lane < lo + width)


def _prompt_attn_kernel(*refs, mode, tq, n_sel, lam_init):
    if mode == 'fox':
        q_ref, k_ref, v_ref, ccol_ref, crow_ref, o_ref, acc_ref, m_ref, l_ref = refs
    elif mode == 'diff':
        (q_ref, k_ref, v_ref, bd_ref, bs_ref, lq1_ref, lk1_ref, lq2_ref, lk2_ref, gd_ref,
         o_ref, acc_ref, m_ref, l_ref) = refs
    else:
        (q_ref, k_ref, v_ref, bd_ref, bs_ref, qi_ref, kx_ref, wi_ref,
         o_ref, acc_ref, m_ref, l_ref, keys_ref, thr_ref) = refs
    qi = pl.program_id(1)
    n_maps = 2 if mode == 'diff' else 1
    scale = (DH_D if mode == 'diff' else HEAD_DIM) ** -0.5
    q = q_ref[0] * scale

    qz = []
    for mp in range(n_maps):
        for h in range(N_HEADS):
            pair = q[:, LANES * (h // 2):LANES * (h // 2 + 1)]
            if mode == 'diff':
                win = _lane_window(HEAD_DIM * (h % 2) + DH_D * mp, DH_D)
            else:
                win = _lane_window(HEAD_DIM * (h % 2), HEAD_DIM)
            qz.append(jnp.where(win, pair, 0.0).astype(BF16))

    acc_ref[...] = jnp.zeros(acc_ref.shape, F32)
    m_ref[...] = jnp.full(m_ref.shape, NEG, F32)
    l_ref[...] = jnp.zeros(l_ref.shape, F32)

    row_id = lax.broadcasted_iota(I32, (tq, tq), 0)
    col_id = lax.broadcasted_iota(I32, (tq, tq), 1)
    causal = col_id <= row_id

    if mode == 'dsa':
        wi = wi_ref[0] * (H_IDX ** -0.5)
        qi_all = qi_ref[0]
        qiz = []
        for h in range(H_IDX):
            pair = qi_all[:, LANES * (h // 2):LANES * (h // 2 + 1)]
            qiz.append(jnp.where(_lane_window(D_IDX * (h % 2), D_IDX), pair, 0.0).astype(BF16))

        def score_block(kb, diag):
            off = pl.multiple_of(kb * tq, tq)
            kx = kx_ref[0, pl.ds(off, tq), :]
            sc = jnp.zeros((tq, tq), F32)
            for h in range(H_IDX):
                s = _mm_nt(qiz[h], kx) * (D_IDX ** -0.5)
                sc = sc + wi[:, h:h + 1] * jnp.maximum(s, 0.0)
            key = _sort_key(sc)
            if diag:
                key = jnp.where(causal, key, INT_MIN)
            keys_ref[:, pl.ds(off, tq)] = key

        def score_body(kb, carry):
            score_block(kb, False)
            return carry

        lax.fori_loop(0, qi, score_body, 0)
        score_block(qi, True)

        def bit_body(it, base):
            cand = base + (jnp.int32(1) << (31 - it))

            def cnt_body(kb, cnt):
                off = pl.multiple_of(kb * tq, tq)
                ge = keys_ref[:, pl.ds(off, tq)] >= cand
                return cnt + jnp.sum(ge.astype(F32), axis=1, keepdims=True)

            cnt = lax.fori_loop(0, qi + 1, cnt_body, jnp.zeros((tq, 1), F32))
            return jnp.where(cnt >= n_sel, cand, base)

        thr_ref[...] = lax.fori_loop(0, 32, bit_body, jnp.full((tq, 1), INT_MIN, I32))

    def step(kb, bias_ref, diag):
        off = pl.multiple_of(kb * tq, tq)
        kblk = k_ref[0, pl.ds(off, tq), :]
        vblk = v_ref[0, pl.ds(off, tq), :]
        if mode == 'dsa':
            sel = keys_ref[:, pl.ds(off, tq)] >= thr_ref[...]
            if diag:
                sel = sel & causal
        for mp in range(n_maps):
            for h in range(N_HEADS):
                idx = mp * N_HEADS + h
                pair = slice(LANES * (h // 2), LANES * (h // 2 + 1))
                s = _mm_nt(qz[idx], kblk[:, pair])
                if mode == 'fox':
                    s = s + (ccol_ref[0, :, h:h + 1] - crow_ref[0, h:h + 1, pl.ds(off, tq)])
                elif bias_ref is not None:
                    s = s + bias_ref[h]
                if mode == 'dsa':
                    s = jnp.where(sel, s, NEG)
                elif diag:
                    s = jnp.where(causal, s, NEG)
                m_prev = m_ref[idx]
                m_new = jnp.maximum(m_prev, jnp.max(s, axis=1, keepdims=True))
                alpha = jnp.exp(m_prev - m_new)
                p = jnp.exp(s - m_new)
                l_ref[idx] = alpha * l_ref[idx] + jnp.sum(p, axis=1, keepdims=True)
                acc_ref[idx] = alpha * acc_ref[idx] + _mm(p.astype(BF16), vblk[:, pair])
                m_ref[idx] = m_new

    def plain_body(kb, carry):
        step(kb, None, False)
        return carry

    if mode == 'fox':
        lax.fori_loop(0, qi, plain_body, 0)
    else:
        lax.fori_loop(0, jnp.maximum(qi - 1, 0), plain_body, 0)

        @pl.when(qi >= 1)
        def _():
            step(qi - 1, bs_ref, False)
    step(qi, None if mode == 'fox' else bd_ref, True)

    if mode == 'diff':
        lam = (jnp.exp(jnp.sum(lq1_ref[...] * lk1_ref[...], axis=1, keepdims=True))
               - jnp.exp(jnp.sum(lq2_ref[...] * lk2_ref[...], axis=1, keepdims=True)) + lam_init)
    outs = []
    for h in range(N_HEADS):
        o = acc_ref[h] / l_ref[h]
        if mode == 'diff':
            o = o - lam * (acc_ref[N_HEADS + h] / l_ref[N_HEADS + h])
            win = _lane_window(HEAD_DIM * (h % 2), HEAD_DIM)
            ms = jnp.sum(jnp.where(win, o * o, 0.0), axis=1, keepdims=True) * (1.0 / HEAD_DIM)
            o = o * lax.rsqrt(ms + EPS) * gd_ref[...] * (1.0 - lam_init)
        outs.append(o)
    lower = _lane_window(0, HEAD_DIM)
    for hp in range(N_HEADS // 2):
        o_ref[0, :, LANES * hp:LANES * (hp + 1)] = jnp.where(lower, outs[2 * hp], outs[2 * hp + 1])


def _prompt_attn(mode, q, k16, v16, extras, tq, n_sel=0, lam_init=0.0):
    b, t, w = q.shape
    n_maps = 2 if mode == 'diff' else 1
    qspec = pl.BlockSpec((1, tq, w), lambda bi, i: (bi, i, 0))
    full = lambda a: pl.BlockSpec((1,) + a.shape[1:], lambda bi, i: (bi,) + (0,) * (a.ndim - 1))
    const = lambda a: pl.BlockSpec(a.shape, lambda bi, i: (0,) * a.ndim)
    scratch = [pltpu.VMEM((n_maps * N_HEADS, tq, LANES), F32),
               pltpu.VMEM((n_maps * N_HEADS, tq, 1), F32),
               pltpu.VMEM((n_maps * N_HEADS, tq, 1), F32)]
    if mode == 'fox':
        ccol, crow = extras
        in_specs = [qspec, full(k16), full(v16), pl.BlockSpec((1, tq, N_HEADS), lambda bi, i: (bi, i, 0)), full(crow)]
        args = (q, k16, v16, ccol, crow)
    elif mode == 'diff':
        bd, bs, lq1, lk1, lq2, lk2, gd = extras
        in_specs = [qspec, full(k16), full(v16)] + [const(a) for a in extras]
        args = (q, k16, v16) + tuple(extras)
    else:
        bd, bs, qi, kx, wi = extras
        in_specs = [qspec, full(k16), full(v16), const(bd), const(bs),
                    pl.BlockSpec((1, tq, qi.shape[2]), lambda bi, i: (bi, i, 0)), full(kx),
                    pl.BlockSpec((1, tq, H_IDX), lambda bi, i: (bi, i, 0))]
        args = (q, k16, v16, bd, bs, qi, kx, wi)
        scratch += [pltpu.VMEM((tq, t), I32), pltpu.VMEM((tq, 1), I32)]
    return pl.pallas_call(
        functools.partial(_prompt_attn_kernel, mode=mode, tq=tq, n_sel=n_sel, lam_init=lam_init),
        grid=(b, t // tq),
        in_specs=in_specs,
        out_specs=qspec,
        out_shape=jax.ShapeDtypeStruct((b, t, w), F32),
        scratch_shapes=scratch,
        compiler_params=_params(("arbitrary", "arbitrary")),
        name="attn_" + mode,
    )(*args)


DEC_ROWS = 16


def _idx_dec_kernel(pt_ref, qi_ref, wi_ref, *refs, n_pg):
    page_refs, o_ref = refs[:n_pg], refs[n_pg]
    qi16 = jnp.concatenate([qi_ref[0], jnp.zeros((DEC_ROWS - H_IDX, D_IDX), F32)], axis=0).astype(BF16)
    wi16 = jnp.concatenate([wi_ref[0] * (H_IDX ** -0.5), jnp.zeros((DEC_ROWS - H_IDX, 1), F32)], axis=0)
    for i in range(n_pg):
        s = _mm_nt(qi16, page_refs[i][...].astype(BF16)) * (D_IDX ** -0.5)
        o_ref[0, i:i + 1, :] = jnp.sum(wi16 * jnp.maximum(s, 0.0), axis=0, keepdims=True)


def _idx_dec(page_table, qi, wi, cache, layer, n_pg):
    db, n_pages = page_table.shape
    page, d = cache.shape[2], cache.shape[3]
    nj = n_pages // n_pg
    page_spec = lambda i: pl.BlockSpec((None, None, page, d), lambda b, j, pt: (layer, pt[b, j * n_pg + i], 0, 0))
    grid_spec = pltpu.PrefetchScalarGridSpec(
        num_scalar_prefetch=1, grid=(db, nj),
        in_specs=[pl.BlockSpec((1, H_IDX, D_IDX), lambda b, j, pt: (b, 0, 0)),
                  pl.BlockSpec((1, H_IDX, 1), lambda b, j, pt: (b, 0, 0))] + [page_spec(i) for i in range(n_pg)],
        out_specs=pl.BlockSpec((1, n_pg, page), lambda b, j, pt: (b, j, 0)))
    return pl.pallas_call(
        functools.partial(_idx_dec_kernel, n_pg=n_pg),
        grid_spec=grid_spec,
        out_shape=jax.ShapeDtypeStruct((db, n_pages, page), F32),
        compiler_params=_params(("arbitrary", "arbitrary")),
        name="idx_dec",
    )(page_table, qi, wi, *([cache] * n_pg))


def _dec_attn_kernel(pt_ref, *refs, mode, n_pg, n_sel, lam_init):
    q_ref, kn_ref, vn_ref = refs[:3]
    kp = refs[3:3 + n_pg]
    vp = refs[3 + n_pg:3 + 2 * n_pg]
    rest = refs[3 + 2 * n_pg:]
    if mode == 'fox':
        lfp = rest[:n_pg]
        fn_ref, bf_ref, o_ref, lfn_ref, qbd_ref, acc_ref, m_ref, l_ref, suf_ref = rest[n_pg:]
    elif mode == 'diff':
        (bias_ref, bnew_ref, lq1_ref, lk1_ref, lq2_ref, lk2_ref, gd_ref,
         o_ref, qbd_ref, acc_ref, m_ref, l_ref) = rest
    else:
        (bias_ref, bnew_ref, sc_ref, qi_ref, kxn_ref, wi_ref,
         o_ref, qbd_ref, acc_ref, m_ref, l_ref, keys_ref, thr_ref) = rest
    j = pl.program_id(1)
    nj = pl.num_programs(1)
    jj = nj - 1 - j
    page = kp[0].shape[0]
    w = q_ref.shape[2]
    scale = (DH_D if mode == 'diff' else HEAD_DIM) ** -0.5

    @pl.when(j == 0)
    def _():
        rowi = lax.broadcasted_iota(I32, (DEC_ROWS, w), 0)
        lane = lax.broadcasted_iota(I32, (DEC_ROWS, w), 1)
        if mode == 'diff':
            hit = (rowi < 2 * N_HEADS) & ((lane // DH_D) == 2 * (rowi % N_HEADS) + rowi // N_HEADS)
        else:
            hit = (rowi < N_HEADS) & ((lane // HEAD_DIM) == rowi)
        qbd = jnp.where(hit, q_ref[0] * scale, 0.0)
        qbd_ref[...] = qbd
        s_new = jnp.sum(qbd * kn_ref[0], axis=1, keepdims=True)
        live = None
        if mode == 'fox':
            lfn = _log_sigmoid(fn_ref[0] + bf_ref[...])
            lfn_ref[0] = lfn
            suf_ref[...] = lfn
        else:
            s_new = s_new + bnew_ref[...]
        if mode == 'dsa':
            wi = wi_ref[0] * (H_IDX ** -0.5)
            s8 = jnp.sum(qi_ref[0] * kxn_ref[0], axis=1, keepdims=True) * (D_IDX ** -0.5)
            sc_new = jnp.sum(wi * jnp.maximum(s8, 0.0), axis=0, keepdims=True)
            key_new = _sort_key(sc_new)
            keys_ref[...] = _sort_key(sc_ref[0])

            def bit_body(it, base):
                cand = base + (jnp.int32(1) << (31 - it))
                ge = (keys_ref[...] >= cand).astype(F32)
                cnt = jnp.sum(jnp.sum(ge, axis=1, keepdims=True), axis=0, keepdims=True)
                cnt = cnt + (key_new >= cand).astype(F32)
                return jnp.where(cnt >= n_sel, cand, base)

            thr = lax.fori_loop(0, 32, bit_body, jnp.full((1, 1), INT_MIN, I32))
            thr_ref[...] = thr
            live = key_new >= thr
        if live is None:
            m_ref[...] = s_new
            l_ref[...] = jnp.ones((DEC_ROWS, 1), F32)
            acc_ref[...] = jnp.broadcast_to(vn_ref[0], (DEC_ROWS, w))
        else:
            m_ref[...] = jnp.where(live, s_new, NEG)
            l_ref[...] = jnp.where(live, 1.0, 0.0) * jnp.ones((DEC_ROWS, 1), F32)
            acc_ref[...] = jnp.where(live, 1.0, 0.0) * jnp.broadcast_to(vn_ref[0], (DEC_ROWS, w))

    qbd = qbd_ref[...].astype(BF16)
    if mode == 'fox':
        r = lax.broadcasted_iota(I32, (page, page), 0)
        c = lax.broadcasted_iota(I32, (page, page), 1)
        later = (r > c).astype(F32)
    for i in reversed(range(n_pg)):
        s = _mm_nt(qbd, kp[i][...].astype(BF16))
        if mode == 'fox':
            lf = lfp[i][...]
            bias = _mm_exact(lf, later) + suf_ref[...]
            suf_ref[...] = suf_ref[...] + jnp.sum(lf, axis=1, keepdims=True)
            s = s + jnp.concatenate([bias, jnp.zeros((DEC_ROWS - 8, page), F32)], axis=0)
        else:
            s = s + bias_ref[:, pl.ds(pl.multiple_of((jj * n_pg + i) * page, page), page)]
        if mode == 'dsa':
            sel = keys_ref[pl.ds(jj * n_pg + i, 1), :] >= thr_ref[...]
            s = jnp.where(sel, s, NEG)
        m_prev = m_ref[...]
        m_new = jnp.maximum(m_prev, jnp.max(s, axis=1, keepdims=True))
        alpha = jnp.exp(m_prev - m_new)
        p = jnp.exp(s - m_new)
        if mode == 'dsa':
            p = jnp.where(sel, p, 0.0)
        l_ref[...] = alpha * l_ref[...] + jnp.sum(p, axis=1, keepdims=True)
        acc_ref[...] = alpha * acc_ref[...] + _mm(p.astype(BF16), vp[i][...].astype(BF16))
        m_ref[...] = m_new

    @pl.when(j == nj - 1)
    def _():
        rowi = lax.broadcasted_iota(I32, (DEC_ROWS, w), 0)
        lane = lax.broadcasted_iota(I32, (DEC_ROWS, w), 1)
        norm = acc_ref[...] / l_ref[...]
        if mode == 'diff':
            lam = (jnp.exp(jnp.sum(lq1_ref[...] * lk1_ref[...], axis=1, keepdims=True))
                   - jnp.exp(jnp.sum(lq2_ref[...] * lk2_ref[...], axis=1, keepdims=True)) + lam_init)
            own = (rowi < 2 * N_HEADS) & ((lane // HEAD_DIM) == (rowi % N_HEADS))
            coef = jnp.where(rowi < N_HEADS, 1.0, -lam)
            o = jnp.sum(jnp.where(own, coef * norm, 0.0), axis=0, keepdims=True)
            seg = ((lax.broadcasted_iota(I32, (w, w), 0) // HEAD_DIM)
                   == (lax.broadcasted_iota(I32, (w, w), 1) // HEAD_DIM)).astype(F32) * (1.0 / HEAD_DIM)
            ms = _mm_exact(jnp.broadcast_to(o * o, (8, w)), seg)[0:1]
            o = o * lax.rsqrt(ms + EPS) * gd_ref[...] * (1.0 - lam_init)
        else:
            own = (rowi < N_HEADS) & ((lane // HEAD_DIM) == rowi)
            o = jnp.sum(jnp.where(own, norm, 0.0), axis=0, keepdims=True)
        o_ref[0] = o


def _dec_attn(mode, page_table, layer, q, knew, vnew, cache_k, cache_v, extras, n_pg, n_sel=0, lam_init=0.0):
    db, n_pages = page_table.shape
    page, w = cache_k.shape[2], cache_k.shape[3]
    nj = n_pages // n_pg
    pg = lambda j, i: (nj - 1 - j) * n_pg + i
    page_spec = lambda i: pl.BlockSpec((None, None, page, w), lambda b, j, pt: (layer, pt[b, pg(j, i)], 0, 0))
    per_seq = lambda a: pl.BlockSpec((1,) + a.shape[1:], lambda b, j, pt: (b,) + (0,) * (a.ndim - 1))
    const = lambda a: pl.BlockSpec(a.shape, lambda b, j, pt: (0,) * a.ndim)
    in_specs = [per_seq(q), per_seq(knew), per_seq(vnew)] + [page_spec(i) for i in range(n_pg)] * 2
    args = [q, knew, vnew] + [cache_k] * n_pg + [cache_v] * n_pg
    out_specs = [per_seq(q)]
    out_shape = [jax.ShapeDtypeStruct(q.shape, F32)]
    scratch = [pltpu.VMEM((DEC_ROWS, w), F32), pltpu.VMEM((DEC_ROWS, w), F32),
               pltpu.VMEM((DEC_ROWS, 1), F32), pltpu.VMEM((DEC_ROWS, 1), F32)]
    if mode == 'fox':
        lf_pages, f_new, b_f = extras
        in_specs += [pl.BlockSpec((None, 8, page), lambda b, j, pt, i=i: (pt[b, pg(j, i)], 0, 0)) for i in range(n_pg)]
        in_specs += [per_seq(f_new), const(b_f)]
        args += [lf_pages] * n_pg + [f_new, b_f]
        out_specs.append(per_seq(f_new))
        out_shape.append(jax.ShapeDtypeStruct(f_new.shape, F32))
        scratch.append(pltpu.VMEM((8, 1), F32))
    elif mode == 'diff':
        in_specs += [const(a) for a in extras]
        args += list(extras)
    else:
        bias, bnew, scores, qi, kxn, wi = extras
        in_specs += [const(bias), const(bnew), per_seq(scores), per_seq(qi), per_seq(kxn), per_seq(wi)]
        args += [bias, bnew, scores, qi, kxn, wi]
        scratch += [pltpu.VMEM((n_pages, page), I32), pltpu.VMEM((1, 1), I32)]
    grid_spec = pltpu.PrefetchScalarGridSpec(
        num_scalar_prefetch=1, grid=(db, nj), in_specs=in_specs, out_specs=out_specs, scratch_shapes=scratch)
    return pl.pallas_call(
        functools.partial(_dec_attn_kernel, mode=mode, n_pg=n_pg, n_sel=n_sel, lam_init=lam_init),
        grid_spec=grid_spec,
        out_shape=out_shape,
        compiler_params=_params(("arbitrary", "arbitrary")),
        name="dec_" + mode,
    )(page_table, *args)


def _out_kernel(x_ref, gpre_ref, wzg_ref, oa_ref, ob_ref, oc_ref, od_ref, wbr_ref, wout_ref, gpost_ref, y_ref):
    x = x_ref[...]
    d = x.shape[1]
    xn = _rms(x, gpre_ref[...]).astype(BF16)
    mixed = jnp.zeros(x.shape, F32)
    for n, o_ref in enumerate((oa_ref, ob_ref, oc_ref, od_ref)):
        z = _mm(xn, wzg_ref[:, BRANCH_W * n:BRANCH_W * (n + 1)])
        hidden = (o_ref[...] * (z * _sigmoid(z))).astype(BF16)
        y_br = _mm(hidden, wbr_ref[n])
        g0 = N_BRANCH * BRANCH_W + d * n
        gate = _mm(xn, wzg_ref[:, g0:g0 + d])
        mixed = mixed + _sigmoid(gate) * y_br
    out = _mm(mixed.astype(BF16), wout_ref[...])
    y_ref[...] = x + _rms(out, gpost_ref[...])


def _out(x2d, gpre, wzg, o_all, wbr, wout, gpost, tm):
    n, d = x2d.shape
    row = lambda w: pl.BlockSpec((tm, w), lambda i: (i, 0))
    const = lambda a: pl.BlockSpec(a.shape, lambda i: (0,) * a.ndim)
    gpre, gpost = gpre.reshape(1, d), gpost.reshape(1, d)
    return pl.pallas_call(
        _out_kernel,
        grid=(n // tm,),
        in_specs=[row(d), const(gpre), const(wzg)] + [row(BRANCH_W)] * N_BRANCH + [const(wbr), const(wout), const(gpost)],
        out_specs=row(d),
        out_shape=jax.ShapeDtypeStruct((n, d), F32),
        compiler_params=_params(("arbitrary",)),
        name="out",
    )(x2d, gpre, wzg, *o_all, wbr, wout, gpost)


def _t5_bucket(rel):
    n = jnp.maximum(rel, 0)
    n_exact = N_BUCKETS // 2
    n_log = jnp.log(jnp.maximum(n, n_exact).astype(F32) / n_exact) / math.log(T5_MAX_DIST / n_exact)
    large = jnp.minimum(n_exact + (n_log * (N_BUCKETS - n_exact)).astype(I32), N_BUCKETS - 1)
    return jnp.where(n < n_exact, n, large)


def _pack_weights(w_in_l):
    cols, off = {}, 0
    for name, width in _SPLITS:
        cols[name] = w_in_l[:, off:off + width]
        off += width
    d = w_in_l.shape[0]
    small = jnp.concatenate([cols['fox_f'], cols['gla_g'], cols['idx_w'],
                             jnp.zeros((d, LANES - 4 - GLA_RANK - H_IDX), w_in_l.dtype)], axis=1)
    parts = [cols['fox_q'], cols['fox_k'], cols['fox_v'], cols['gla_q'], cols['gla_k'], cols['gla_v'],
             cols['dsa_q'], cols['dsa_k'], cols['dsa_v'], cols['idx_q'], cols['idx_k'], cols['idx_k'],
             cols['diff_q'], cols['diff_k'], cols['diff_v'], small]
    wa = jnp.concatenate(parts, axis=1).astype(BF16)
    wzg = jnp.concatenate([cols['z'], cols['gate']], axis=1).astype(BF16)
    return wa, wzg


def _cut(u, name):
    off, width = _PACK[name]
    return u[..., off:off + width]


def kernel(x_prompt, x_sample, cache_fox_k, cache_fox_v, cache_fox_logf, cache_dsa_k, cache_dsa_v, cache_dsa_idxk, cache_diff_k, cache_diff_v, state_gla, page_table, norm_pre, norm_post, w_in, fox_b_f, gla_w_gate, gla_b_gate, gla_norm, diff_lambda_q1, diff_lambda_k1, diff_lambda_q2, diff_lambda_k2, diff_norm, w_branch, w_out, rel_bias):
    b, t, d = x_prompt.shape
    db, dt, _ = x_sample.shape
    depth = w_in.shape[0]
    n_pool, page = cache_fox_k.shape[1], cache_fox_k.shape[2]
    n_pages = page_table.shape[1]
    past = n_pages * page
    assert dt == 1 and d == 1024
    tq = min(256, t)
    assert t % tq == 0 and tq >= T5_MAX_DIST
    n_sel_p = min(TOPK_MAX, t // 4)
    n_sel_s = min(TOPK_MAX, (past + dt) // 4)
    n_pg = 8 if n_pages % 8 == 0 else 1
    tm_p = min(256, b * t)
    gla_pad = LANES

    dist = jnp.arange(tq)[:, None] - jnp.arange(tq)[None, :]
    bkt_diag = _t5_bucket(dist)
    bkt_sub = _t5_bucket(dist + tq)
    far = rel_bias[N_BUCKETS - 1]
    tile = lambda bkt, lo: jnp.moveaxis(rel_bias[bkt][..., lo:lo + N_HEADS] - far[lo:lo + N_HEADS], -1, 0)
    bd_dsa, bs_dsa = tile(bkt_diag, 0), tile(bkt_sub, 0)
    bd_dif, bs_dif = tile(bkt_diag, N_HEADS), tile(bkt_sub, N_HEADS)
    bkt_dec = _t5_bucket(past - jnp.arange(past))
    pad_rows = lambda a: jnp.concatenate([a, jnp.zeros((DEC_ROWS - a.shape[0],) + a.shape[1:], F32)], axis=0)
    dec_tab = rel_bias[bkt_dec].T
    bias_dsa = pad_rows(dec_tab[:N_HEADS])
    bias_dif = pad_rows(jnp.concatenate([dec_tab[N_HEADS:], dec_tab[N_HEADS:]], axis=0))
    new_tab = rel_bias[0][:, None]
    bnew_dsa = pad_rows(new_tab[:N_HEADS])
    bnew_dif = pad_rows(jnp.concatenate([new_tab[N_HEADS:], new_tab[N_HEADS:]], axis=0))

    c_fox_k = cache_fox_k.reshape(depth, n_pool, page, BRANCH_W)
    c_fox_v = cache_fox_v.reshape(depth, n_pool, page, BRANCH_W)
    c_dsa_k = cache_dsa_k.reshape(depth, n_pool, page, BRANCH_W)
    c_dsa_v = cache_dsa_v.reshape(depth, n_pool, page, BRANCH_W)
    c_dif_k = cache_diff_k.reshape(depth, n_pool, page, BRANCH_W)
    c_dif_v = cache_diff_v.reshape(depth, n_pool, page, BRANCH_W)

    xp = x_prompt.reshape(b * t, d)
    xs = x_sample.reshape(db, d)
    rows_p, rows_s = [], []
    for l in range(depth):
        lam_init = 0.8 - 0.6 * math.exp(-0.3 * l)
        wa, wzg = _pack_weights(w_in[l])
        wbr = w_branch[l].astype(BF16)
        wout = w_out[l].astype(BF16)
        wg_pad = jnp.zeros((LANES, H_B * DK_B), F32).at[SMALL_GLA_G:SMALL_GLA_G + GLA_RANK].set(gla_w_gate[l])
        bg = gla_b_gate[l].reshape(1, -1)
        gn = jnp.tile(gla_norm[l], H_B).reshape(1, -1)
        gd2 = jnp.tile(diff_norm[l], 2).reshape(1, -1)
        gd4 = jnp.tile(diff_norm[l], N_HEADS).reshape(1, -1)
        lams = tuple(a[l].reshape(1, -1) for a in (diff_lambda_q1, diff_lambda_k1, diff_lambda_q2, diff_lambda_k2))
        b_f8 = jnp.concatenate([fox_b_f[l], jnp.zeros((8 - N_HEADS,), F32)]).reshape(8, 1)

        u = _proj(xp, norm_pre[l], wa, tm_p).reshape(b, t, PACK_W)
        small = _cut(u, 'small')
        f_t = jnp.swapaxes(small[..., SMALL_FOX_F:SMALL_FOX_F + 8], 1, 2)
        logf_t, c_t = _foxgate(f_t, b_f8)
        o_a = _prompt_attn('fox', _cut(u, 'fox_q'), _cut(u, 'fox_k').astype(BF16), _cut(u, 'fox_v').astype(BF16),
                           (jnp.swapaxes(c_t[:, :N_HEADS], 1, 2), c_t), tq)
        o_b, s_p = _gla(_cut(u, 'gla_q'), _cut(u, 'gla_k'), _cut(u, 'gla_v'), small, wg_pad, bg, gn,
                        jnp.zeros((b, H_B, DK_B, DV_B), F32), tq, None)
        o_c = _prompt_attn('dsa', _cut(u, 'dsa_q'), _cut(u, 'dsa_k').astype(BF16), _cut(u, 'dsa_v').astype(BF16),
                           (bd_dsa, bs_dsa, _cut(u, 'idx_q'), _cut(u, 'idx_k2').astype(BF16),
                            small[..., SMALL_IDX_W:SMALL_IDX_W + H_IDX]), tq, n_sel=n_sel_p)
        o_d = _prompt_attn('diff', _cut(u, 'diff_q'), _cut(u, 'diff_k').astype(BF16), _cut(u, 'diff_v').astype(BF16),
                           (bd_dif, bs_dif) + lams + (gd2,), tq, lam_init=lam_init)
        heads = lambda a: a.reshape(b, t, N_HEADS, HEAD_DIM)
        rows_p.append((heads(_cut(u, 'fox_k')), heads(_cut(u, 'fox_v')), jnp.swapaxes(logf_t[:, :N_HEADS], 1, 2),
                       heads(_cut(u, 'dsa_k')), heads(_cut(u, 'dsa_v')), _cut(u, 'idx_k2')[..., :D_IDX],
                       heads(_cut(u, 'diff_k')), heads(_cut(u, 'diff_v')), s_p))
        o_all = [o.reshape(b * t, BRANCH_W) for o in (o_a, o_b, o_c, o_d)]
        xp = _out(xp, norm_pre[l], wzg, o_all, wbr, wout, norm_post[l], tm_p)

        us = _proj(xs, norm_pre[l], wa, db)
        seq = lambda name: _cut(us, name).reshape(db, 1, -1)
        small_s = _cut(us, 'small')
        f_new = small_s[:, SMALL_FOX_F:SMALL_FOX_F + 8].reshape(db, 8, 1)
        lf_pages = jnp.pad(jnp.swapaxes(cache_fox_logf[l], 1, 2), ((0, 0), (0, 8 - N_HEADS), (0, 0)))
        so_a, lf_new = _dec_attn('fox', page_table, l, seq('fox_q'), seq('fox_k'), seq('fox_v'), c_fox_k, c_fox_v,
                                 (lf_pages, f_new, b_f8), n_pg)
        pad_t = lambda a: jnp.pad(a.reshape(db, 1, -1), ((0, 0), (0, gla_pad - 1), (0, 0)))
        so_b, s_s = _gla(pad_t(_cut(us, 'gla_q')), pad_t(_cut(us, 'gla_k')), pad_t(_cut(us, 'gla_v')), pad_t(small_s),
                         wg_pad, bg, gn, state_gla[l], gla_pad, 1)
        qi_s = _cut(us, 'idx_q').reshape(db, H_IDX, D_IDX)
        wi_s = small_s[:, SMALL_IDX_W:SMALL_IDX_W + H_IDX].reshape(db, H_IDX, 1)
        kx_new = _cut(us, 'idx_k2')[:, :D_IDX].reshape(db, 1, D_IDX)
        scores = _idx_dec(page_table, qi_s, wi_s, cache_dsa_idxk, l, n_pg)
        so_c, = _dec_attn('dsa', page_table, l, seq('dsa_q'), seq('dsa_k'), seq('dsa_v'), c_dsa_k, c_dsa_v,
                          (bias_dsa, bnew_dsa, scores, qi_s, kx_new, wi_s), n_pg, n_sel=n_sel_s)
        so_d, = _dec_attn('diff', page_table, l, seq('diff_q'), seq('diff_k'), seq('diff_v'), c_dif_k, c_dif_v,
                          (bias_dif, bnew_dif) + lams + (gd4,), n_pg, lam_init=lam_init)
        heads_s = lambda name: _cut(us, name).reshape(db, 1, N_HEADS, HEAD_DIM)
        rows_s.append((heads_s('fox_k'), heads_s('fox_v'), lf_new[:, :N_HEADS, 0].reshape(db, 1, N_HEADS),
                       heads_s('dsa_k'), heads_s('dsa_v'), kx_new,
                       heads_s('diff_k'), heads_s('diff_v'), s_s))
        o_all_s = [so_a.reshape(db, BRANCH_W), so_b[:, 0, :], so_c.reshape(db, BRANCH_W), so_d.reshape(db, BRANCH_W)]
        xs = _out(xs, norm_pre[l], wzg, o_all_s, wbr, wout, norm_post[l], db)

    sp = [jnp.stack(r) for r in zip(*rows_p)]
    ss = [jnp.stack(r) for r in zip(*rows_s)]
    return (xp.reshape(b, t, d), xs.reshape(db, dt, d), sp[0], sp[1], sp[2], sp[3], sp[4], sp[5], sp[6], sp[7], sp[8],
            ss[0], ss[1], ss[2], ss[3], ss[4], ss[5], ss[6], ss[7], ss[8])
```
